```python
import math
import jax, jax.numpy as jnp
from jax import lax
import numpy as np

D_MODEL = 1024
BATCH = 16
SEQ = 2048
DEPTH = 1
DEC_BATCH = 32
DEC_SEQ = 8
PAST_LEN = 16384
PAGE_SIZE = 128

D_RNN = D_MODEL
RNN_BLOCKS = 16
RNN_BLOCK_W = D_RNN // RNN_BLOCKS
CONV_W = 4
LRU_C = 8.0
HEAD_DIM = 64
GROUPS = ((128, 1), (512, 4), (2048, 16))
HEADS_PER_GROUP = 4
N_HEADS = HEADS_PER_GROUP * len(GROUPS)
ATT_W = N_HEADS * HEAD_DIM
ATT_OUT_W = HEADS_PER_GROUP * HEAD_DIM
ROPE_THETA = 10000.0
D_FF = ((8 * D_MODEL + 2) // 3 + 255) // 256 * 256
IN_W = 2 * D_RNN + 3 * ATT_W + 2 * D_MODEL
EPS = 1e-6
NEG = -1e30

kernel_name = "hybrid_rglru_dilated_swa_decode_step"


def rms_norm(x, g):
    xf = x.astype(jnp.float32)
    y = xf * lax.rsqrt(jnp.mean(xf * xf, axis=-1, keepdims=True) + EPS) * g.astype(jnp.float32)
    return y.astype(x.dtype)


def rope(x, pos):
    half = HEAD_DIM // 2
    inv = jnp.exp(-math.log(ROPE_THETA) * jnp.arange(half, dtype=jnp.float32) * (2.0 / HEAD_DIM))
    ang = pos.astype(jnp.float32)[:, None] * inv[None, :]
    c = jnp.cos(ang)[None, :, None, :]
    s = jnp.sin(ang)[None, :, None, :]
    xf = x.astype(jnp.float32)
    x1, x2 = xf[..., :half], xf[..., half:]
    return jnp.concatenate([x1 * c - x2 * s, x2 * c + x1 * s], axis=-1).astype(x.dtype)


def causal_conv(x, buf, w, b):
    T = x.shape[1]
    xp = jnp.concatenate([buf.astype(x.dtype), x], axis=1)
    y = b + sum(xp[:, j:j + T] * w[j] for j in range(CONV_W))
    return y, xp[:, -(CONV_W - 1):]


def rg_lru(x, h0, gate_a_w, gate_a_b, gate_x_w, gate_x_b, lam):
    B, T, _ = x.shape
    xb = x.reshape(B, T, RNN_BLOCKS, RNN_BLOCK_W)
    r = jax.nn.sigmoid(jnp.einsum('btnc,ncd->btnd', xb, gate_a_w).reshape(B, T, D_RNN) + gate_a_b)
    i = jax.nn.sigmoid(jnp.einsum('btnc,ncd->btnd', xb, gate_x_w).reshape(B, T, D_RNN) + gate_x_b)
    log_a = -LRU_C * r.astype(jnp.float32) * jax.nn.softplus(-lam.astype(jnp.float32))
    a = jnp.exp(log_a)
    u = jnp.sqrt(-jnp.expm1(2.0 * log_a)) * (i * x).astype(jnp.float32)

    def step(h, au):
        a_t, u_t = au
        h = a_t * h + u_t
        return h, h

    hT, hs = lax.scan(step, h0.astype(jnp.float32), (jnp.swapaxes(a, 0, 1), jnp.swapaxes(u, 0, 1)))
    return jnp.swapaxes(hs, 0, 1).astype(x.dtype), hT.astype(h0.dtype)


def dilated_attn_prompt(q, k, v, dil, nback):
    B, S, H, Dh = q.shape
    L = S // dil
    nb = -(-L // nback)
    Lp = nb * nback

    def to_cls(t):
        t = t.reshape(B, L, dil, H, Dh).transpose(0, 2, 1, 3, 4)
        return jnp.pad(t, ((0, 0), (0, 0), (0, Lp - L), (0, 0), (0, 0)))

    def windows(t):
        t = jnp.pad(t, ((0, 0), (0, 0), (nback, 0), (0, 0), (0, 0))).reshape(B, dil, nb + 1, nback, H, Dh)
        return jnp.concatenate([t[:, :, :-1], t[:, :, 1:]], axis=3)

    qb = to_cls(q).reshape(B, dil, nb, nback, H, Dh)
    kw = windows(to_cls(k))
    vw = windows(to_cls(v))
    s = jnp.einsum('brnqhc,brnkhc->brnhqk', qb, kw, preferred_element_type=jnp.float32) * (HEAD_DIM ** -0.5)
    qi = jnp.arange(nback)[:, None]
    kj = jnp.arange(2 * nback)[None, :]
    dist = nback + qi - kj
    key_idx = jnp.arange(nb)[:, None, None] * nback + kj[None] - nback
    valid = (dist >= 0)[None] & (dist <= nback)[None] & (key_idx >= 0)
    s = jnp.where(valid[:, None], s, NEG)
    lse = jax.nn.logsumexp(s, axis=-1)
    p = jnp.exp(s - lse[..., None])
    o = jnp.einsum('brnhqk,brnkhc->brnqhc', p.astype(v.dtype), vw, preferred_element_type=jnp.float32)
    o = o.reshape(B, dil, Lp, H, Dh)[:, :, :L].transpose(0, 2, 1, 3, 4).reshape(B, S, H, Dh)
    lse = lse.transpose(0, 1, 2, 4, 3).reshape(B, dil, Lp, H)[:, :, :L].transpose(0, 2, 1, 3).reshape(B, S, H)
    return o, lse


def dilated_attn_sample(q, k_all, v_all, dil, nback):
    B, T, H, Dh = q.shape
    Wb = k_all.shape[1] - T
    idx = Wb + jnp.arange(T)[:, None] - jnp.arange(nback + 1)[None, :] * dil
    valid = idx >= 0
    idxc = jnp.clip(idx, 0)
    kg = k_all[:, idxc]
    vg = v_all[:, idxc]
    s = jnp.einsum('bthc,btmhc->bthm', q, kg, preferred_element_type=jnp.float32) * (HEAD_DIM ** -0.5)
    s = jnp.where(valid[:, None, :], s, NEG)
    lse = jax.nn.logsumexp(s, axis=-1)
    p = jnp.exp(s - lse[..., None])
    o = jnp.einsum('bthm,btmhc->bthc', p.astype(vg.dtype), vg, preferred_element_type=jnp.float32)
    return o, lse


def decoder_layer(x, pos, conv_buf, h0, kv_bufs, norm1_g, w_in, b_merge, conv_w, conv_b,
                  gate_a_w, gate_a_b, gate_x_w, gate_x_b, rg_lambda, w_branch_a, w_branch_b,
                  w_out, norm2_g, w_ffn_in, w_ffn_out):
    B, T, _ = x.shape
    xn = rms_norm(x, norm1_g)
    z = xn @ w_in
    cuts = [D_RNN, 2 * D_RNN, 2 * D_RNN + ATT_W, 2 * D_RNN + 2 * ATT_W, 2 * D_RNN + 3 * ATT_W]
    x_rnn, g_rnn, q, k, v, g_merge = jnp.split(z, cuts, axis=-1)
    xc, new_conv = causal_conv(x_rnn, conv_buf, conv_w, conv_b)
    hs, new_h = rg_lru(xc, h0, gate_a_w, gate_a_b, gate_x_w, gate_x_b, rg_lambda)
    branch_a = (jax.nn.gelu(g_rnn) * hs) @ w_branch_a
    q = rope(q.reshape(B, T, N_HEADS, HEAD_DIM), pos)
    k = rope(k.reshape(B, T, N_HEADS, HEAD_DIM), pos)
    v = v.reshape(B, T, N_HEADS, HEAD_DIM)
    outs, lses, new_kv = [], [], []
    for g, (win, dil) in enumerate(GROUPS):
        hsl = slice(g * HEADS_PER_GROUP, (g + 1) * HEADS_PER_GROUP)
        qg, kg, vg = q[:, :, hsl], k[:, :, hsl], v[:, :, hsl]
        nback = win // dil
        if kv_bufs is None:
            o, l = dilated_attn_prompt(qg, kg, vg, dil, nback)
            keep = min(win, T)
            new_kv.append((kg[:, -keep:], vg[:, -keep:]))
        else:
            kb, vb = kv_bufs[g]
            k_all = jnp.concatenate([kb.astype(kg.dtype), kg], axis=1)
            v_all = jnp.concatenate([vb.astype(vg.dtype), vg], axis=1)
            o, l = dilated_attn_sample(qg, k_all, v_all, dil, nback)
            keep = kb.shape[1]
            new_kv.append((k_all[:, -keep:], v_all[:, -keep:]))
        outs.append(o)
        lses.append(l)
    wts = jax.nn.softmax(jnp.stack(lses), axis=0)
    o = jnp.sum(wts[..., None] * jnp.stack(outs), axis=0).reshape(B, T, ATT_OUT_W).astype(x.dtype)
    branch_b = o @ w_branch_b
    gates = jax.nn.sigmoid((g_merge + b_merge).astype(jnp.float32)).astype(x.dtype)
    g_a, g_b = jnp.split(gates, 2, axis=-1)
    h = x + (g_a * branch_a + g_b * branch_b) @ w_out
    hn = rms_norm(h, norm2_g)
    f_gate, f_up = jnp.split(hn @ w_ffn_in, 2, axis=-1)
    h = h + (jax.nn.silu(f_gate) * f_up) @ w_ffn_out
    return h, new_conv, new_h, new_kv


def setup_inputs(seed: int = 0) -> dict:
    key = jax.random.key(seed)
    ks = iter(jax.random.split(key, 40))

    def nrm(shape, scale):
        return jax.random.normal(next(ks), shape, jnp.float32) * scale

    inp = {}
    inp['x_prompt'] = nrm((BATCH, SEQ, D_MODEL), 1.0)
    inp['x_sample'] = nrm((DEC_BATCH, DEC_SEQ, D_MODEL), 1.0)
    inp['state_conv'] = nrm((DEPTH, DEC_BATCH, CONV_W - 1, D_RNN), 0.5)
    inp['state_h'] = nrm((DEPTH, DEC_BATCH, D_RNN), 0.5)
    for g, (win, dil) in enumerate(GROUPS):
        wb = min(win, PAST_LEN)
        inp['cache_k_g%d' % g] = nrm((DEPTH, DEC_BATCH, wb, HEADS_PER_GROUP, HEAD_DIM), 1.0)
        inp['cache_v_g%d' % g] = nrm((DEPTH, DEC_BATCH, wb, HEADS_PER_GROUP, HEAD_DIM), 1.0)
    inp['norm1_g'] = 1.0 + nrm((DEPTH, D_MODEL), 0.01)
    inp['w_in'] = nrm((DEPTH, D_MODEL, IN_W), D_MODEL ** -0.5)
    inp['b_merge'] = nrm((DEPTH, 2 * D_MODEL), 0.01)
    inp['conv_w'] = nrm((DEPTH, CONV_W, D_RNN), CONV_W ** -0.5)
    inp['conv_b'] = nrm((DEPTH, D_RNN), 0.01)
    inp['gate_a_w'] = nrm((DEPTH, RNN_BLOCKS, RNN_BLOCK_W, RNN_BLOCK_W), RNN_BLOCK_W ** -0.5)
    inp['gate_a_b'] = nrm((DEPTH, D_RNN), 0.01)
    inp['gate_x_w'] = nrm((DEPTH, RNN_BLOCKS, RNN_BLOCK_W, RNN_BLOCK_W), RNN_BLOCK_W ** -0.5)
    inp['gate_x_b'] = nrm((DEPTH, D_RNN), 0.01)
    u = jax.random.uniform(next(ks), (DEPTH, D_RNN), jnp.float32, 0.9, 0.999)
    sa = u ** (1.0 / LRU_C)
    inp['rg_lambda'] = jnp.log(sa) - jnp.log1p(-sa)
    inp['w_branch_a'] = nrm((DEPTH, D_RNN, D_MODEL), D_RNN ** -0.5)
    inp['w_branch_b'] = nrm((DEPTH, ATT_OUT_W, D_MODEL), ATT_OUT_W ** -0.5)
    inp['w_out'] = nrm((DEPTH, D_MODEL, D_MODEL), D_MODEL ** -0.5)
    inp['norm2_g'] = 1.0 + nrm((DEPTH, D_MODEL), 0.01)
    inp['w_ffn_in'] = nrm((DEPTH, D_MODEL, 2 * D_FF), D_MODEL ** -0.5)
    inp['w_ffn_out'] = nrm((DEPTH, D_FF, D_MODEL), D_FF ** -0.5)
    inp['norm_f_g'] = 1.0 + nrm((D_MODEL,), 0.01)
    return inp


def reference(x_prompt, x_sample, state_conv, state_h, cache_k_g0, cache_v_g0, cache_k_g1,
              cache_v_g1, cache_k_g2, cache_v_g2, norm1_g, w_in, b_merge, conv_w, conv_b,
              gate_a_w, gate_a_b, gate_x_w, gate_x_b, rg_lambda, w_branch_a, w_branch_b,
              w_out, norm2_g, w_ffn_in, w_ffn_out, norm_f_g):
    pos_p = jnp.arange(x_prompt.shape[1], dtype=jnp.int32)
    pos_s = PAST_LEN + jnp.arange(x_sample.shape[1], dtype=jnp.int32)
    Bp = x_prompt.shape[0]
    xp, xs = x_prompt, x_sample
    p_conv, p_h, s_conv, s_h = [], [], [], []
    p_kv = [[[], []] for _ in GROUPS]
    s_kv = [[[], []] for _ in GROUPS]
    for l in range(DEPTH):
        lw = (norm1_g[l], w_in[l], b_merge[l], conv_w[l], conv_b[l], gate_a_w[l], gate_a_b[l],
              gate_x_w[l], gate_x_b[l], rg_lambda[l], w_branch_a[l], w_branch_b[l], w_out[l],
              norm2_g[l], w_ffn_in[l], w_ffn_out[l])
        conv0 = jnp.zeros((Bp, CONV_W - 1, D_RNN), xp.dtype)
        h0 = jnp.zeros((Bp, D_RNN), state_h.dtype)
        xp, cp, hp, kvp = decoder_layer(xp, pos_p, conv0, h0, None, *lw)
        bufs = [(cache_k_g0[l], cache_v_g0[l]), (cache_k_g1[l], cache_v_g1[l]), (cache_k_g2[l], cache_v_g2[l])]
        xs, cs, hs, kvs = decoder_layer(xs, pos_s, state_conv[l], state_h[l], bufs, *lw)
        p_conv.append(cp)
        p_h.append(hp)
        s_conv.append(cs)
        s_h.append(hs)
        for g in range(len(GROUPS)):
            p_kv[g][0].append(kvp[g][0])
            p_kv[g][1].append(kvp[g][1])
            s_kv[g][0].append(kvs[g][0])
            s_kv[g][1].append(kvs[g][1])
    y_prompt = rms_norm(xp, norm_f_g)
    y_sample = rms_norm(xs, norm_f_g)
    st = jnp.stack
    return (y_prompt, y_sample,
            st(p_conv), st(p_h),
            st(p_kv[0][0]), st(p_kv[0][1]), st(p_kv[1][0]), st(p_kv[1][1]), st(p_kv[2][0]), st(p_kv[2][1]),
            st(s_conv), st(s_h),
            st(s_kv[0][0]), st(s_kv[0][1]), st(s_kv[1][0]), st(s_kv[1][1]), st(s_kv[2][0]), st(s_kv[2][1]))
```

```python
import functools
import math

import jax
import jax.numpy as jnp
from jax import lax
from jax.experimental import pallas as pl
from jax.experimental.pallas import tpu as pltpu

F32 = jnp.float32
BF16 = jnp.bfloat16

D_MODEL = 1024
D_RNN = D_MODEL
RNN_BLOCKS = 16
RNN_BLOCK_W = D_RNN // RNN_BLOCKS
CONV_W = 4
LRU_C = 8.0
HEAD_DIM = 64
GROUPS = ((128, 1), (512, 4), (2048, 16))
N_GROUPS = len(GROUPS)
HEADS_PER_GROUP = 4
GROUP_W = HEADS_PER_GROUP * HEAD_DIM
ATT_W = N_GROUPS * GROUP_W
NBACK = 128
ROPE_THETA = 10000.0
D_FF = 2816
PAST_LEN = 16384
EPS = 1e-6
NEG = -1e30

OFF_RNN = 0
OFF_GELU = D_RNN
OFF_Q = 2 * D_RNN
OFF_K = OFF_Q + ATT_W
OFF_V = OFF_K + ATT_W
OFF_MERGE = OFF_V + ATT_W
IN_W = OFF_MERGE + 2 * D_MODEL

LANES = 128
SUBLANES = 8
MXU_TILE = 256
VMEM_LIMIT_BYTES = 56 * 1024 * 1024

_RESIDENT = dict(pipeline_mode=pl.Buffered(1))


def _params(*semantics):
    return pltpu.CompilerParams(dimension_semantics=semantics, vmem_limit_bytes=VMEM_LIMIT_BYTES)


def _log2(n):
    assert n > 0 and n & (n - 1) == 0, "power of two expected"
    return n.bit_length() - 1


def _full_spec(shape):
    return pl.BlockSpec(shape, lambda *_: (0,) * len(shape), **_RESIDENT)


def _rms(x, g):
    return x * lax.rsqrt(jnp.mean(x * x, axis=-1, keepdims=True) + EPS) * g


def _in_proj_kernel(x_ref, g_ref, w_ref, bm_ref, cos_ref, sin_ref,
                    xr_ref, gg_ref, q0, q1, q2, k0, k1, k2, v0, v1, v2, gates_ref):
    tm = x_ref.shape[0]
    xn = _rms(x_ref[...], g_ref[...]).astype(BF16)

    def proj(lo, width):
        return jnp.dot(xn, w_ref[:, lo:lo + width], preferred_element_type=F32)

    xr_ref[...] = proj(OFF_RNN, D_RNN)
    gg_ref[...] = jax.nn.gelu(proj(OFF_GELU, D_RNN))

    cos = cos_ref[...]
    sin = sin_ref[...]
    lane = lax.broadcasted_iota(jnp.int32, (tm, LANES), 1)
    first_half = (lane & (HEAD_DIM - 1)) < (HEAD_DIM // 2)

    def rope(z):
        partner = jnp.where(first_half, pltpu.roll(z, LANES - HEAD_DIM // 2, 1),
                            pltpu.roll(z, HEAD_DIM // 2, 1))
        return z * cos + partner * sin

    scale = HEAD_DIM ** -0.5
    for g, (q_ref, k_ref, v_ref) in enumerate(((q0, k0, v0), (q1, k1, v1), (q2, k2, v2))):
        zq = proj(OFF_Q + g * GROUP_W, GROUP_W)
        zk = proj(OFF_K + g * GROUP_W, GROUP_W)
        for c in range(GROUP_W // LANES):
            sl = slice(c * LANES, (c + 1) * LANES)
            q_ref[:, sl] = rope(zq[:, sl]) * scale
            k_ref[:, sl] = rope(zk[:, sl])
        v_ref[...] = proj(OFF_V + g * GROUP_W, GROUP_W)

    gates_ref[...] = jax.nn.sigmoid(proj(OFF_MERGE, 2 * D_MODEL) + bm_ref[...])


def _in_proj(x2d, norm_g, w_in_bf, b_merge, cos_tab, sin_tab, tm):
    rows = x2d.shape[0]
    n_tab_blocks = cos_tab.shape[0] // tm
    row_spec = lambda w: pl.BlockSpec((tm, w), lambda i: (i, 0))
    tab_spec = pl.BlockSpec((tm, LANES), lambda i: (i % n_tab_blocks, 0))
    out_widths = [D_RNN, D_RNN] + [GROUP_W] * 9 + [2 * D_MODEL]
    return pl.pallas_call(
        _in_proj_kernel,
        grid=(rows // tm,),
        in_specs=[row_spec(D_MODEL), _full_spec((1, D_MODEL)), _full_spec((D_MODEL, IN_W)),
                  _full_spec((1, 2 * D_MODEL)), tab_spec, tab_spec],
        out_specs=[row_spec(w) for w in out_widths],
        out_shape=[jax.ShapeDtypeStruct((rows, w), F32) for w in out_widths],
        compiler_params=_params("parallel"),
        name="in_proj",
    )(x2d, norm_g, w_in_bf, b_merge, cos_tab, sin_tab)


def _rnn_kernel(xr_ref, gg_ref, cbuf_ref, h0_ref, cw_ref, cb_ref, wa_ref, ba_ref, wx_ref, bx_ref,
                lam_ref, wbr_ref, out_ref, hT_ref, xbuf, a_s, u_s, h_s, hcar):
    i = pl.program_id(1)
    tt = xr_ref.shape[0]
    halo = SUBLANES

    @pl.when(i == 0)
    def _():
        xbuf[0:halo, :] = jnp.zeros((halo, D_RNN), F32)
        xbuf[halo - (CONV_W - 1):halo, :] = cbuf_ref[...]
        hcar[...] = jnp.broadcast_to(h0_ref[...], (SUBLANES, D_RNN))

    x = xr_ref[...]
    xbuf[halo:halo + tt, :] = x
    xc = cb_ref[...] + x * cw_ref[CONV_W - 1:CONV_W, :]
    for j in range(CONV_W - 1):
        d = CONV_W - 1 - j
        xc = xc + xbuf[halo - d:halo - d + tt, :] * cw_ref[j:j + 1, :]
    xbuf[0:halo, :] = xbuf[tt:tt + halo, :]

    xcb = xc.astype(BF16)
    lam = -lam_ref[...]
    softplus_neg_lam = jnp.maximum(lam, 0.0) + jnp.log1p(jnp.exp(-jnp.abs(lam)))
    for c in range(D_RNN // MXU_TILE):
        sl = slice(c * MXU_TILE, (c + 1) * MXU_TILE)
        xs = xcb[:, sl]
        r = jax.nn.sigmoid(jnp.dot(xs, wa_ref[c], preferred_element_type=F32) + ba_ref[:, sl])
        ig = jax.nn.sigmoid(jnp.dot(xs, wx_ref[c], preferred_element_type=F32) + bx_ref[:, sl])
        log_a = -LRU_C * r * softplus_neg_lam[:, sl]
        a = jnp.exp(log_a)
        a_s[:, sl] = a
        u_s[:, sl] = jnp.sqrt(1.0 - jnp.exp(2.0 * log_a)) * (ig * xc[:, sl])

    row = lax.broadcasted_iota(jnp.int32, (SUBLANES, D_RNN), 0)

    def slab(j, hprev):
        rows = pl.ds(pl.multiple_of(j * SUBLANES, SUBLANES), SUBLANES)
        a = a_s[rows, :]
        u = u_s[rows, :]
        for s in (1, 2, 4):
            keep = row >= s
            a_prev = pltpu.roll(a, s, 0)
            u_prev = pltpu.roll(u, s, 0)
            u = jnp.where(keep, a * u_prev + u, u)
            a = jnp.where(keep, a * a_prev, a)
        h = a * hprev + u
        h_s[rows, :] = h
        return jnp.broadcast_to(h[SUBLANES - 1:SUBLANES, :], (SUBLANES, D_RNN))

    hcar[...] = lax.fori_loop(0, tt // SUBLANES, slab, hcar[...])

    gated = (gg_ref[...] * h_s[...]).astype(BF16)
    out_ref[...] = jnp.dot(gated, wbr_ref[...], preferred_element_type=F32)

    @pl.when(i == pl.num_programs(1) - 1)
    def _():
        hT_ref[...] = hcar[0:1, :]


def _rnn(xr, gg, conv_buf, h0, conv_w, conv_b, wa_bd, ba, wx_bd, bx, lam, w_branch_a_bf, tt):
    nb, t, _ = xr.shape
    seq_spec = pl.BlockSpec((None, tt, D_RNN), lambda b, i: (b, i, 0))
    n_tiles = D_RNN // MXU_TILE
    return pl.pallas_call(
        _rnn_kernel,
        grid=(nb, t // tt),
        in_specs=[seq_spec, seq_spec,
                  pl.BlockSpec((None, CONV_W - 1, D_RNN), lambda b, i: (b, 0, 0)),
                  pl.BlockSpec((None, 1, D_RNN), lambda b, i: (b, 0, 0)),
                  _full_spec((CONV_W, D_RNN)), _full_spec((1, D_RNN)),
                  _full_spec((n_tiles, MXU_TILE, MXU_TILE)), _full_spec((1, D_RNN)),
                  _full_spec((n_tiles, MXU_TILE, MXU_TILE)), _full_spec((1, D_RNN)),
                  _full_spec((1, D_RNN)), _full_spec((D_RNN, D_MODEL))],
        out_specs=[pl.BlockSpec((None, tt, D_MODEL), lambda b, i: (b, i, 0)),
                   pl.BlockSpec((None, 1, D_RNN), lambda b, i: (b, 0, 0))],
        out_shape=[jax.ShapeDtypeStruct((nb, t, D_MODEL), F32),
                   jax.ShapeDtypeStruct((nb, 1, D_RNN), F32)],
        scratch_shapes=[pltpu.VMEM((tt + SUBLANES, D_RNN), F32), pltpu.VMEM((tt, D_RNN), F32),
                        pltpu.VMEM((tt, D_RNN), F32), pltpu.VMEM((tt, D_RNN), F32),
                        pltpu.VMEM((SUBLANES, D_RNN), F32)],
        compiler_params=_params("parallel", "arbitrary"),
        name="rnn",
    )(xr, gg, conv_buf, h0, conv_w, conv_b, wa_bd, ba, wx_bd, bx, lam, w_branch_a_bf)


def _softmax_pv(s, v):
    m = jnp.max(s, axis=-1, keepdims=True)
    p = jnp.exp(s - m)
    l = jnp.sum(p, axis=-1, keepdims=True)
    acc = jnp.dot(p.astype(BF16), v, preferred_element_type=F32)
    return acc / l, m + jnp.log(l)


def _attn_prompt_kernel(q_ref, k_ref, v_ref, o_ref, lse_ref, *, n_keys):
    i = pl.program_id(2)
    qb = q_ref.shape[0]
    start = pl.multiple_of(jnp.maximum(i * qb + qb - n_keys, 0), LANES)
    q = q_ref[...].astype(BF16)
    k = k_ref[pl.ds(start, n_keys), :].astype(BF16)
    v = v_ref[pl.ds(start, n_keys), :].astype(BF16)
    qpos = i * qb - start + lax.broadcasted_iota(jnp.int32, (qb, n_keys), 0)
    kpos = lax.broadcasted_iota(jnp.int32, (qb, n_keys), 1)
    valid = (kpos <= qpos) & (kpos >= qpos - NBACK)
    for h in range(HEADS_PER_GROUP):
        sl = slice(h * HEAD_DIM, (h + 1) * HEAD_DIM)
        s = lax.dot_general(q[:, sl], k[:, sl], (((1,), (1,)), ((), ())),
                            preferred_element_type=F32)
        o, lse = _softmax_pv(jnp.where(valid, s, NEG), v[:, sl])
        o_ref[:, sl] = o
        lse_ref[:, sl] = jnp.broadcast_to(lse, (qb, HEAD_DIM))


def _attn_prompt(q, k, v, dil):
    nb, s, _ = q.shape
    cls_len = s // dil
    qb = NBACK
    n_keys = min(2 * NBACK, cls_len)
    view = lambda t: t.reshape(nb, cls_len, dil * GROUP_W)
    q_spec = pl.BlockSpec((None, qb, GROUP_W), lambda b, r, i: (b, i, r))
    kv_spec = pl.BlockSpec((None, cls_len, GROUP_W), lambda b, r, i: (b, 0, r))
    o, lse = pl.pallas_call(
        functools.partial(_attn_prompt_kernel, n_keys=n_keys),
        grid=(nb, dil, cls_len // qb),
        in_specs=[q_spec, kv_spec, kv_spec],
        out_specs=[q_spec, q_spec],
        out_shape=[jax.ShapeDtypeStruct((nb, cls_len, dil * GROUP_W), F32)] * 2,
        compiler_params=_params("parallel", "parallel", "arbitrary"),
        name="attn_prompt_d%d" % dil,
    )(view(q), view(k), view(v))
    return o.reshape(nb, s, GROUP_W), lse.reshape(nb, s, GROUP_W)


def _attn_sample_kernel(q_ref, kn_ref, vn_ref, kc_ref, vc_ref, o_ref, lse_ref, kout_ref, vout_ref,
                        kall, vall, *, dil):
    t_new = q_ref.shape[0]
    win = kc_ref.shape[0]
    pad = jnp.zeros((LANES - t_new, GROUP_W), F32)
    for c_ref, n_ref, out_ref, all_ref in ((kc_ref, kn_ref, kout_ref, kall),
                                           (vc_ref, vn_ref, vout_ref, vall)):
        new = n_ref[...]
        out_ref[0:win - t_new, :] = c_ref[t_new:win, :]
        out_ref[win - t_new:win, :] = new
        all_ref[0:win, :] = c_ref[...].astype(BF16)
        all_ref[win:win + LANES, :] = jnp.concatenate([new, pad], axis=0).astype(BF16)

    rows = HEADS_PER_GROUP * t_new
    head_of_row = lax.broadcasted_iota(jnp.int32, (rows, GROUP_W), 0) >> _log2(t_new)
    head_of_lane = lax.broadcasted_iota(jnp.int32, (rows, GROUP_W), 1) >> _log2(HEAD_DIM)
    own = head_of_row == head_of_lane
    q4 = jnp.concatenate([q_ref[...]] * HEADS_PER_GROUP, axis=0)
    qx = jnp.where(own, q4, 0.0).astype(BF16)
    s = lax.dot_general(qx, kall[...], (((1,), (1,)), ((), ())), preferred_element_type=F32)
    n_all = win + LANES
    t = lax.broadcasted_iota(jnp.int32, (rows, n_all), 0) & (t_new - 1)
    back = win + t - lax.broadcasted_iota(jnp.int32, (rows, n_all), 1)
    valid = (back >= 0) & (back <= NBACK * dil) & ((back & (dil - 1)) == 0)
    o_all, lse = _softmax_pv(jnp.where(valid, s, NEG), vall[...])
    o_own = jnp.where(own, o_all, 0.0)
    lse_own = jnp.where(own, jnp.broadcast_to(lse, (rows, GROUP_W)), 0.0)
    o = o_own[0:t_new]
    l = lse_own[0:t_new]
    for h in range(1, HEADS_PER_GROUP):
        o = o + o_own[h * t_new:(h + 1) * t_new]
        l = l + lse_own[h * t_new:(h + 1) * t_new]
    o_ref[...] = o
    lse_ref[...] = l


def _attn_sample(q, k_new, v_new, cache_k, cache_v, dil):
    nb, t_new, _ = q.shape
    win = cache_k.shape[1]
    new_spec = pl.BlockSpec((None, t_new, GROUP_W), lambda b: (b, 0, 0))
    cache_spec = pl.BlockSpec((None, win, GROUP_W), lambda b: (b, 0, 0))
    return pl.pallas_call(
        functools.partial(_attn_sample_kernel, dil=dil),
        grid=(nb,),
        in_specs=[new_spec, new_spec, new_spec, cache_spec, cache_spec],
        out_specs=[new_spec, new_spec, cache_spec, cache_spec],
        out_shape=[jax.ShapeDtypeStruct((nb, t_new, GROUP_W), F32)] * 2
        + [jax.ShapeDtypeStruct((nb, win, GROUP_W), F32)] * 2,
        scratch_shapes=[pltpu.VMEM((win + LANES, GROUP_W), BF16)] * 2,
        compiler_params=_params("parallel"),
        name="attn_sample_d%d" % dil,
    )(q, k_new, v_new, cache_k, cache_v)


FF_CHUNK = 256


def _final_kernel(x_ref, ba_ref, gates_ref, o0, o1, o2, l0, l1, l2, wb_ref, wo_ref, g2_ref,
                  wfi_ref, wfo_ref, gf_ref, y_ref):
    ls = [l0[...], l1[...], l2[...]]
    m = jnp.maximum(jnp.maximum(ls[0], ls[1]), ls[2])
    es = [jnp.exp(l - m) for l in ls]
    den = es[0] + es[1] + es[2]
    o = (es[0] * o0[...] + es[1] * o1[...] + es[2] * o2[...]) / den
    branch_b = jnp.dot(o.astype(BF16), wb_ref[...], preferred_element_type=F32)
    gates = gates_ref[...]
    merged = gates[:, :D_MODEL] * ba_ref[...] + gates[:, D_MODEL:] * branch_b
    h = x_ref[...] + jnp.dot(merged.astype(BF16), wo_ref[...], preferred_element_type=F32)
    hn = _rms(h, g2_ref[...]).astype(BF16)
    acc = h
    for c in range(D_FF // FF_CHUNK):
        lo = c * FF_CHUNK
        f_gate = jnp.dot(hn, wfi_ref[:, lo:lo + FF_CHUNK], preferred_element_type=F32)
        f_up = jnp.dot(hn, wfi_ref[:, D_FF + lo:D_FF + lo + FF_CHUNK], preferred_element_type=F32)
        act = (jax.nn.silu(f_gate) * f_up).astype(BF16)
        acc = acc + jnp.dot(act, wfo_ref[lo:lo + FF_CHUNK, :], preferred_element_type=F32)
    y_ref[...] = _rms(acc, gf_ref[...])


def _final(x2d, branch_a, gates, os_, lses, wb, wo, g2, wfi, wfo, gf, tm):
    rows = x2d.shape[0]
    row_spec = lambda w: pl.BlockSpec((tm, w), lambda i: (i, 0))
    return pl.pallas_call(
        _final_kernel,
        grid=(rows // tm,),
        in_specs=[row_spec(D_MODEL), row_spec(D_MODEL), row_spec(2 * D_MODEL)]
        + [row_spec(GROUP_W)] * 6
        + [_full_spec((GROUP_W, D_MODEL)), _full_spec((D_MODEL, D_MODEL)), _full_spec((1, D_MODEL)),
           _full_spec((D_MODEL, 2 * D_FF)), _full_spec((D_FF, D_MODEL)), _full_spec((1, D_MODEL))],
        out_specs=row_spec(D_MODEL),
        out_shape=jax.ShapeDtypeStruct((rows, D_MODEL), F32),
        compiler_params=_params("parallel"),
        name="final",
    )(x2d, branch_a, gates, *os_, *lses, wb, wo, g2, wfi, wfo, gf)


def _rope_tables(pos):
    half = HEAD_DIM // 2
    inv = jnp.exp(-math.log(ROPE_THETA) * jnp.arange(half, dtype=F32) * (2.0 / HEAD_DIM))
    ang = pos.astype(F32)[:, None] * inv[None, :]
    c, s = jnp.cos(ang), jnp.sin(ang)
    reps = LANES // HEAD_DIM
    return jnp.tile(jnp.concatenate([c, c], axis=1), (1, reps)), \
        jnp.tile(jnp.concatenate([-s, s], axis=1), (1, reps))


def _block_diag_tiles(w):
    per_tile = MXU_TILE // RNN_BLOCK_W
    w = w.reshape(D_RNN // MXU_TILE, per_tile, RNN_BLOCK_W, RNN_BLOCK_W)
    eye = jnp.eye(per_tile, dtype=w.dtype)
    t = jnp.einsum('tpcd,pq->tpcqd', w, eye)
    return t.reshape(D_RNN // MXU_TILE, MXU_TILE, MXU_TILE)


def _layer(x, pos_tab, conv_buf, h0, caches, lw, tm_in, tm_final, tt):
    nb, t, _ = x.shape
    rows = nb * t
    x2d = x.reshape(rows, D_MODEL)
    outs = _in_proj(x2d, lw['norm1_g'], lw['w_in'], lw['b_merge'], pos_tab[0], pos_tab[1], tm_in)
    xr, gg = outs[0], outs[1]
    qs, ks, vs = outs[2:5], outs[5:8], outs[8:11]
    gates = outs[11]

    xr3 = xr.reshape(nb, t, D_RNN)
    branch_a, h_last = _rnn(xr3, gg.reshape(nb, t, D_RNN), conv_buf, h0.reshape(nb, 1, D_RNN),
                            lw['conv_w'], lw['conv_b'], lw['gate_a_w'], lw['gate_a_b'],
                            lw['gate_x_w'], lw['gate_x_b'], lw['rg_lambda'], lw['w_branch_a'], tt)
    new_conv = xr3[:, t - (CONV_W - 1):, :]

    os_, lses, new_kv = [], [], []
    for g, (win, dil) in enumerate(GROUPS):
        q3, k3, v3 = (a.reshape(nb, t, GROUP_W) for a in (qs[g], ks[g], vs[g]))
        if caches is None:
            o, lse = _attn_prompt(q3, k3, v3, dil)
            keep = min(win, t)
            new_kv += [k3[:, t - keep:], v3[:, t - keep:]]
        else:
            ck, cv = caches[g]
            o, lse, nk, nv = _attn_sample(q3, k3, v3, ck, cv, dil)
            new_kv += [nk, nv]
        os_.append(o.reshape(rows, GROUP_W))
        lses.append(lse.reshape(rows, GROUP_W))

    y = _final(x2d, branch_a.reshape(rows, D_MODEL), gates, os_, lses, lw['w_branch_b'], lw['w_out'],
               lw['norm2_g'], lw['w_ffn_in'], lw['w_ffn_out'], lw['norm_f_g'], tm_final)
    heads = lambda a: a.reshape(1, nb, a.shape[1], HEADS_PER_GROUP, HEAD_DIM)
    return (y.reshape(nb, t, D_MODEL), new_conv[None], h_last.reshape(1, nb, D_RNN),
            [heads(a) for a in new_kv])


def kernel(x_prompt, x_sample, state_conv, state_h, cache_k_g0, cache_v_g0, cache_k_g1, cache_v_g1, cache_k_g2, cache_v_g2, norm1_g, w_in, b_merge, conv_w, conv_b, gate_a_w, gate_a_b, gate_x_w, gate_x_b, rg_lambda, w_branch_a, w_branch_b, w_out, norm2_g, w_ffn_in, w_ffn_out, norm_f_g):
    assert w_in.shape[0] == 1, "single-layer trunk"
    nbp, seq, _ = x_prompt.shape
    nbs, dec_seq, _ = x_sample.shape
    assert seq % (NBACK * GROUPS[-1][1]) == 0 and dec_seq >= CONV_W - 1
    lw = dict(
        norm1_g=norm1_g[0][None], w_in=w_in[0].astype(BF16), b_merge=b_merge[0][None],
        conv_w=conv_w[0], conv_b=conv_b[0][None],
        gate_a_w=_block_diag_tiles(gate_a_w[0]).astype(BF16), gate_a_b=gate_a_b[0][None],
        gate_x_w=_block_diag_tiles(gate_x_w[0]).astype(BF16), gate_x_b=gate_x_b[0][None],
        rg_lambda=rg_lambda[0][None], w_branch_a=w_branch_a[0].astype(BF16),
        w_branch_b=w_branch_b[0].astype(BF16), w_out=w_out[0].astype(BF16),
        norm2_g=norm2_g[0][None], w_ffn_in=w_ffn_in[0].astype(BF16),
        w_ffn_out=w_ffn_out[0].astype(BF16), norm_f_g=norm_f_g[None])

    tab_p = _rope_tables(jnp.arange(seq, dtype=jnp.int32))
    pos_s = PAST_LEN + jnp.arange(dec_seq, dtype=jnp.int32)
    tab_s = tuple(jnp.tile(t, (nbs, 1)) for t in _rope_tables(pos_s))

    yp, p_conv, p_h, p_kv = _layer(
        x_prompt, tab_p, jnp.zeros((nbp, CONV_W - 1, D_RNN), F32), jnp.zeros((nbp, D_RNN), F32),
        None, lw, tm_in=512, tm_final=256, tt=256)
    caches = [(ck[0].reshape(nbs, -1, GROUP_W), cv[0].reshape(nbs, -1, GROUP_W))
              for ck, cv in ((cache_k_g0, cache_v_g0), (cache_k_g1, cache_v_g1),
                             (cache_k_g2, cache_v_g2))]
    ys, s_conv, s_h, s_kv = _layer(
        x_sample, tab_s, state_conv[0], state_h[0], caches, lw,
        tm_in=nbs * dec_seq, tm_final=nbs * dec_seq, tt=dec_seq)
    return (yp, ys, p_conv, p_h, *p_kv, s_conv, s_h, *s_kv)
```
